```python
import jax, jax.numpy as jnp
from jax import lax
import numpy as np


D_MODEL = 1024
BATCH = 16
SEQ = 2048
DEPTH = 2

D_MIX = D_MODEL
HG_WIDTH = D_MIX // 2
HG_HEADS = 4
HG_DK = HG_WIDTH // HG_HEADS
HG_DV = HG_WIDTH // HG_HEADS
HG_CHUNK = 16
AT_WIDTH = D_MIX - HG_WIDTH
AT_HEADS = 8
AT_DH = AT_WIDTH // AT_HEADS
MOBA_BLOCK = 256
MOBA_TOPK = 3
MOBA_QCHUNK = 16
IN_SPLITS = (HG_WIDTH, HG_WIDTH, HG_WIDTH, HG_WIDTH, AT_WIDTH, AT_WIDTH, AT_WIDTH, AT_WIDTH)
IN_COLS = sum(IN_SPLITS)
NORM_EPS = 1e-6
MASK_VALUE = -1e30
LB_MAX = 1.0 - 1e-6

kernel_name = "hymba_style_hgrn2_moba_block"


def _rms_norm(x, w):
    xf = x.astype(jnp.float32)
    y = xf * lax.rsqrt(jnp.mean(xf * xf, axis=-1, keepdims=True) + NORM_EPS)
    return (y * w.astype(jnp.float32)).astype(x.dtype)


def _alibi_slopes(n_heads):
    return jnp.asarray([2.0 ** (-8.0 * (h + 1) / n_heads) for h in range(n_heads)], dtype=jnp.float32)


def _hgrn_lower_bounds(lb_logits):
    p = jax.nn.softmax(lb_logits.astype(jnp.float32), axis=0)
    return jnp.clip(jnp.cumsum(p, axis=0) - p[0:1], 0.0, LB_MAX)


def _hgrn2(q, f_logit, i, lb):
    B, T, _ = q.shape
    C = HG_CHUNK
    N = T // C

    def heads(z):
        return z.astype(jnp.float32).reshape(B, N, C, HG_HEADS, -1).transpose(0, 3, 1, 2, 4)

    q = jax.nn.silu(heads(q))
    lbh = lb.astype(jnp.float32).reshape(HG_HEADS, 1, 1, HG_DK)
    f = lbh + (1.0 - lbh) * jax.nn.sigmoid(heads(f_logit))
    log_f = jnp.logaddexp(jnp.log(lbh), jnp.log1p(-lbh) + jax.nn.log_sigmoid(heads(f_logit)))
    k = 1.0 - f
    i = heads(i)
    b = jnp.cumsum(log_f, axis=3)
    b_last = b[:, :, :, C - 1:]
    causal = jnp.tril(jnp.ones((C, C), dtype=bool))[:, :, None]
    pair = b[:, :, :, :, None, :] - b[:, :, :, None, :, :]
    decay_ts = jnp.exp(jnp.where(causal, pair, -jnp.inf))
    scores = jnp.einsum("bhncd,bhnsd,bhncsd->bhncs", q, k, decay_ts)
    o_intra = jnp.einsum("bhncs,bhnse->bhnce", scores, i)
    q_in = q * jnp.exp(b)
    k_st = k * jnp.exp(b_last - b)
    decay = jnp.exp(b_last[:, :, :, 0])

    def step(S, xs):
        qn, kn, in_, dn = xs
        o = jnp.einsum("bhcd,bhde->bhce", qn, S)
        S = dn[..., None] * S + jnp.einsum("bhcd,bhce->bhde", kn, in_)
        return S, o

    S0 = jnp.zeros((B, HG_HEADS, HG_DK, HG_DV), jnp.float32)
    mv = lambda z: jnp.moveaxis(z, 2, 0)
    _, o_inter = lax.scan(step, S0, (mv(q_in), mv(k_st), mv(i), mv(decay)))
    o = o_intra + jnp.moveaxis(o_inter, 0, 2)
    return o.transpose(0, 2, 3, 1, 4).reshape(B, T, HG_HEADS, HG_DV)


def _moba(q, k, v, q_gain, k_gain):
    B, T, _ = q.shape
    heads = lambda z: z.reshape(B, T, AT_HEADS, AT_DH).transpose(0, 2, 1, 3)
    q = _rms_norm(heads(q), q_gain).astype(jnp.float32)
    k = _rms_norm(heads(k), k_gain).astype(jnp.float32)
    v = heads(v).astype(jnp.float32)

    nb = -(-T // MOBA_BLOCK)
    t_pad = nb * MOBA_BLOCK
    pad = ((0, 0), (0, 0), (0, t_pad - T), (0, 0))
    k_pad = jnp.pad(k, pad)
    v_pad = jnp.pad(v, pad)
    kb = k_pad.reshape(B, AT_HEADS, nb, MOBA_BLOCK, AT_DH)
    vb = v_pad.reshape(B, AT_HEADS, nb, MOBA_BLOCK, AT_DH)

    k_mean = jnp.mean(kb, axis=3)
    cur_blk = jnp.arange(T) // MOBA_BLOCK
    past = jnp.arange(nb)[None, :] < cur_blk[:, None]
    gate = jnp.where(past, jnp.einsum("bhtd,bhnd->bhtn", q, k_mean), MASK_VALUE)
    n_sel = min(MOBA_TOPK, nb)
    _, sel = lax.top_k(gate, n_sel)
    valid = sel < cur_blk[:, None]

    nq = T // MOBA_QCHUNK
    chunk = lambda z: jnp.moveaxis(z.reshape(B, AT_HEADS, nq, MOBA_QCHUNK, *z.shape[3:]), 2, 0)
    t0s = jnp.arange(nq, dtype=jnp.int32) * MOBA_QCHUNK
    slopes = _alibi_slopes(AT_HEADS)[None, :, None, None]
    bi = jnp.arange(B)[:, None, None, None]
    hi = jnp.arange(AT_HEADS)[None, :, None, None]
    offs = jnp.arange(MOBA_BLOCK)
    scale = AT_DH ** -0.5

    def attend(xs):
        qc, selc, validc, t0 = xs
        t = t0 + jnp.arange(MOBA_QCHUNK)
        blk0 = (t0 // MOBA_BLOCK) * MOBA_BLOCK
        k_own = lax.dynamic_slice_in_dim(k_pad, blk0, MOBA_BLOCK, axis=2)
        v_own = lax.dynamic_slice_in_dim(v_pad, blk0, MOBA_BLOCK, axis=2)
        dist_own = (t[:, None] - (blk0 + offs)[None, :]).astype(jnp.float32)
        s_own = jnp.einsum("bhqd,bhsd->bhqs", qc, k_own) * scale - slopes * dist_own
        s_own = jnp.where(dist_own >= 0, s_own, MASK_VALUE)
        k_sel = kb[bi, hi, selc]
        v_sel = vb[bi, hi, selc]
        dist_sel = (t[:, None, None] - (selc[..., None] * MOBA_BLOCK + offs)).astype(jnp.float32)
        s_sel = jnp.einsum("bhqd,bhqjsd->bhqjs", qc, k_sel) * scale - slopes[..., None] * dist_sel
        s_sel = jnp.where(validc[..., None], s_sel, MASK_VALUE)
        s_all = jnp.concatenate(
            [s_own, s_sel.reshape(B, AT_HEADS, MOBA_QCHUNK, n_sel * MOBA_BLOCK)], axis=-1)
        p = jax.nn.softmax(s_all, axis=-1)
        p_own = p[..., :MOBA_BLOCK]
        p_sel = p[..., MOBA_BLOCK:].reshape(B, AT_HEADS, MOBA_QCHUNK, n_sel, MOBA_BLOCK)
        return (jnp.einsum("bhqs,bhsd->bhqd", p_own, v_own)
                + jnp.einsum("bhqjs,bhqjsd->bhqd", p_sel, v_sel))

    out = lax.map(attend, (chunk(q), chunk(sel), chunk(valid), t0s))
    out = jnp.moveaxis(out, 0, 2).reshape(B, AT_HEADS, T, AT_DH)
    return out.transpose(0, 2, 1, 3).reshape(B, T, AT_WIDTH)


def setup_inputs(seed: int = 0) -> dict:
    key = jax.random.key(seed)
    ks = jax.random.split(key, 8)
    x = jax.random.normal(ks[0], (BATCH, SEQ, D_MODEL), jnp.float32)
    norm_w = 1.0 + 0.02 * jax.random.normal(ks[1], (DEPTH, D_MODEL), jnp.float32)
    w_in = jax.random.normal(ks[2], (DEPTH, D_MODEL, IN_COLS), jnp.float32) * D_MODEL ** -0.5
    hg_lb_logits = 0.5 * jax.random.normal(ks[3], (DEPTH, HG_WIDTH), jnp.float32)
    hg_norm_w = 1.0 + 0.02 * jax.random.normal(ks[4], (DEPTH, HG_DV), jnp.float32)
    q_norm_w = 1.0 + 0.02 * jax.random.normal(ks[5], (DEPTH, AT_DH), jnp.float32)
    k_norm_w = 1.0 + 0.02 * jax.random.normal(ks[6], (DEPTH, AT_DH), jnp.float32)
    w_out = jax.random.normal(ks[7], (DEPTH, D_MIX, D_MODEL), jnp.float32) * D_MIX ** -0.5
    return {"x": x, "norm_w": norm_w, "w_in": w_in, "hg_lb_logits": hg_lb_logits,
            "hg_norm_w": hg_norm_w, "q_norm_w": q_norm_w, "k_norm_w": k_norm_w, "w_out": w_out}


def reference(x, norm_w, w_in, hg_lb_logits, hg_norm_w, q_norm_w, k_norm_w, w_out):
    B, T, _ = x.shape
    lower_bounds = _hgrn_lower_bounds(hg_lb_logits)
    split_points = [int(s) for s in np.cumsum(IN_SPLITS)[:-1]]
    for l in range(DEPTH):
        h = _rms_norm(x, norm_w[l])
        proj = jnp.einsum("btd,dc->btc", h, w_in[l])
        hq, hf, hi, hg, aq, ak, av, ag = jnp.split(proj, split_points, axis=-1)
        o_h = _rms_norm(_hgrn2(hq, hf, hi, lower_bounds[l]), hg_norm_w[l]).reshape(B, T, HG_WIDTH)
        o_h = o_h * jax.nn.silu(hg.astype(jnp.float32))
        o_a = _moba(aq, ak, av, q_norm_w[l], k_norm_w[l]) * jax.nn.silu(ag.astype(jnp.float32))
        mixed = jnp.concatenate([o_h, o_a], axis=-1).astype(x.dtype)
        x = x + jnp.einsum("btc,cd->btd", mixed, w_out[l])
    return x
```

```python
import functools

import jax
import jax.numpy as jnp
from jax import lax
from jax.experimental import pallas as pl
from jax.experimental.pallas import tpu as pltpu

F32 = jnp.float32
BF16 = jnp.bfloat16

D_MODEL = 1024
HG_WIDTH = 512
HG_HEADS = 4
HG_D = 128
AT_WIDTH = 512
AT_HEADS = 8
AT_DH = 64
MOBA_BLOCK = 256
MOBA_TOPK = 3
IN_COLS = 4096
NORM_EPS = 1e-6
MASK_VALUE = -1e30
LB_MAX = 1.0 - 1e-6

LANES = 128
SUBLANES = 8
HG_CHUNK = 128
PROJ_ROWS = 512
PROJ_COLS = 512
OUT_ROWS = 1024
VMEM_LIMIT = 48 * 1024 * 1024

COL_HQ, COL_HF, COL_HI, COL_HG = 0, 4, 8, 12
COL_AQ, COL_AK, COL_AV, COL_AG = 16, 20, 24, 28

NT_DIMS = (((1,), (1,)), ((), ()))
TN_DIMS = (((0,), (0,)), ((), ()))


def _sigmoid(x):
    e = jnp.exp(-jnp.abs(x))
    return jnp.where(x >= 0, 1.0, e) / (1.0 + e)


def _silu(x):
    return x * _sigmoid(x)


def _proj_kernel(x_ref, nw_ref, w_ref, pb_ref, hf_ref):
    x = x_ref[...]
    ms = jnp.mean(x * x, axis=-1, keepdims=True)
    h = (x * lax.rsqrt(ms + NORM_EPS) * nw_ref[...]).astype(BF16)
    for c in range(IN_COLS // PROJ_COLS):
        cols = slice(c * PROJ_COLS, (c + 1) * PROJ_COLS)
        r = jnp.dot(h, w_ref[:, cols], preferred_element_type=F32)
        pb_ref[:, cols] = r.astype(BF16)
        if c == (COL_HF * LANES) // PROJ_COLS:
            hf_ref[...] = r


def _proj(x2d, nw, w_bf16):
    n = x2d.shape[0]
    return pl.pallas_call(
        _proj_kernel,
        grid=(n // PROJ_ROWS,),
        in_specs=[
            pl.BlockSpec((PROJ_ROWS, D_MODEL), lambda i: (i, 0)),
            pl.BlockSpec((1, D_MODEL), lambda i: (0, 0)),
            pl.BlockSpec((D_MODEL, IN_COLS), lambda i: (0, 0)),
        ],
        out_specs=[
            pl.BlockSpec((PROJ_ROWS, IN_COLS), lambda i: (i, 0)),
            pl.BlockSpec((PROJ_ROWS, HG_WIDTH), lambda i: (i, 0)),
        ],
        out_shape=[
            jax.ShapeDtypeStruct((n, IN_COLS), BF16),
            jax.ShapeDtypeStruct((n, HG_WIDTH), F32),
        ],
        compiler_params=pltpu.CompilerParams(
            dimension_semantics=("arbitrary",), vmem_limit_bytes=VMEM_LIMIT),
        name="norm_in_proj",
    )(x2d, nw, w_bf16)


def _split3(x):
    h1 = x.astype(BF16)
    r1 = x - h1.astype(F32)
    h2 = r1.astype(BF16)
    r2 = r1 - h2.astype(F32)
    return h1, h2, r2.astype(BF16)


def _boundary_rows(b, m):
    L = b.shape[0]
    nv = L // SUBLANES
    b3 = b.reshape(nv, SUBLANES, LANES)

    def sub(j):
        return jnp.broadcast_to(b3[:, j:j + 1, :], (nv, SUBLANES, LANES))

    if m >= SUBLANES:
        last = sub(SUBLANES - 1)
        half = m // SUBLANES
        parts = []
        for p in range(nv // (2 * half)):
            r = p * 2 * half + half - 1
            parts.append(jnp.broadcast_to(last[r:r + 1], (2 * half, SUBLANES, LANES)))
        e3 = parts[0] if len(parts) == 1 else jnp.concatenate(parts, axis=0)
    else:
        s = lax.broadcasted_iota(jnp.int32, (nv, SUBLANES, LANES), 1)
        e3 = sub(m - 1)
        for start in range(2 * m, SUBLANES, 2 * m):
            e3 = jnp.where(s >= start, sub(start + m - 1), e3)
    return e3.reshape(L, LANES)


def _hgrn_kernel(hq_ref, hf_ref, hi_ref, hg_ref, lb_ref, nw_ref, o_ref):
    T = hq_ref.shape[0]
    L = HG_CHUNK
    lb = lb_ref[0:1, :]
    log_lb = lb_ref[1:2, :]
    log1m_lb = lb_ref[2:3, :]
    nw = nw_ref[...]

    row = lax.broadcasted_iota(jnp.int32, (L, L), 0)
    col = lax.broadcasted_iota(jnp.int32, (L, L), 1)
    tri = (row >= col).astype(BF16)
    x = row ^ col
    level = jnp.full((L, L), -1, jnp.int32)
    m = 1
    while m < L:
        level = level + (x >= m).astype(jnp.int32)
        m *= 2
    level = jnp.where(col > row, -2, level)
    rowl = lax.broadcasted_iota(jnp.int32, (L, LANES), 0)

    def chunk(c, st):
        rows = pl.ds(pl.multiple_of(c * L, L), L)
        qr = hq_ref[rows, :].astype(F32)
        z = hf_ref[rows, :]
        v = hi_ref[rows, :]
        q = _silu(qr)
        e = jnp.exp(-jnp.abs(z))
        sig = jnp.where(z >= 0, 1.0, e) / (1.0 + e)
        log_sig = jnp.minimum(z, 0.0) - jnp.log(1.0 + e)
        f = lb + (1.0 - lb) * sig
        k = 1.0 - f
        a1 = log_lb
        a2 = log1m_lb + log_sig
        log_f = jnp.maximum(a1, a2) + jnp.log(1.0 + jnp.exp(-jnp.abs(a1 - a2)))

        h1, h2, h3 = _split3(log_f)
        b = (jnp.dot(tri, h1, preferred_element_type=F32)
             + jnp.dot(tri, h2, preferred_element_type=F32)
             + jnp.dot(tri, h3, preferred_element_type=F32))
        b_last = b[L - 1:L, :]

        q_in = (q * jnp.exp(b)).astype(BF16)
        o = lax.dot_general(q_in, st.astype(BF16), NT_DIMS, preferred_element_type=F32)

        a = jnp.where(level == -1, jnp.sum(q * k, axis=-1, keepdims=True), 0.0)
        m = 1
        lg = 0
        while m < L:
            w = jnp.exp(-jnp.abs(b - _boundary_rows(b, m)))
            second = (rowl & m) != 0
            qt = (q * jnp.where(second, w, 0.0)).astype(BF16)
            kt = (k * jnp.where(second, 0.0, w)).astype(BF16)
            al = lax.dot_general(qt, kt, NT_DIMS, preferred_element_type=F32)
            a = jnp.where(level == lg, al, a)
            m *= 2
            lg += 1
        o = o + jnp.dot(a.astype(BF16), v, preferred_element_type=F32)

        k_st = (k * jnp.exp(b_last - b)).astype(BF16)
        st = st * jnp.exp(b_last) + lax.dot_general(v, k_st, TN_DIMS, preferred_element_type=F32)

        ms = jnp.mean(o * o, axis=-1, keepdims=True)
        on = o * lax.rsqrt(ms + NORM_EPS) * nw
        g = hg_ref[rows, :].astype(F32)
        o_ref[rows, :] = (on * _silu(g)).astype(BF16)
        return st

    lax.fori_loop(0, T // L, chunk, jnp.zeros((HG_D, HG_D), F32))


def _hgrn(pb, hf, lbs, nw, batch, seq):
    n = pb.shape[0]
    blk = lambda off: pl.BlockSpec((seq, LANES), lambda b, h: (b, off + h))
    return pl.pallas_call(
        _hgrn_kernel,
        grid=(batch, HG_HEADS),
        in_specs=[
            blk(COL_HQ),
            pl.BlockSpec((seq, LANES), lambda b, h: (b, h)),
            blk(COL_HI),
            blk(COL_HG),
            pl.BlockSpec((3, LANES), lambda b, h: (0, h)),
            pl.BlockSpec((1, LANES), lambda b, h: (0, 0)),
        ],
        out_specs=pl.BlockSpec((seq, LANES), lambda b, h: (b, h)),
        out_shape=jax.ShapeDtypeStruct((n, HG_WIDTH), BF16),
        compiler_params=pltpu.CompilerParams(
            dimension_semantics=("arbitrary", "arbitrary"), vmem_limit_bytes=VMEM_LIMIT),
        name="hgrn2",
    )(pb, hf, pb, pb, lbs, nw)


def _moba_kernel(q_ref, k_ref, v_ref, g_ref, qg_ref, kg_ref, sl_ref, o_ref,
                 qn_s, kn_s, vt_s, bias_s):
    T = q_ref.shape[0]
    nb = T // MOBA_BLOCK
    BLK = MOBA_BLOCK
    lane = lax.broadcasted_iota(jnp.int32, (BLK, LANES), 1)
    head0 = lane < AT_DH

    def rms_heads(x, gain):
        x2 = x * x
        s0 = jnp.sum(jnp.where(head0, x2, 0.0), axis=-1, keepdims=True)
        s1 = jnp.sum(jnp.where(head0, 0.0, x2), axis=-1, keepdims=True)
        inv = jnp.where(head0, lax.rsqrt(s0 / AT_DH + NORM_EPS), lax.rsqrt(s1 / AT_DH + NORM_EPS))
        return x * inv * gain

    kmeans = []
    for j in range(nb):
        rows = slice(j * BLK, (j + 1) * BLK)
        qn = rms_heads(q_ref[rows, :].astype(F32), qg_ref[...]) * (AT_DH ** -0.5)
        qn_s[0, rows, :] = jnp.where(head0, qn, 0.0).astype(BF16)
        qn_s[1, rows, :] = jnp.where(head0, 0.0, qn).astype(BF16)
        kn = rms_heads(k_ref[rows, :].astype(F32), kg_ref[...])
        kn_s[rows, :] = kn.astype(BF16)
        kmeans.append(jnp.mean(kn, axis=0, keepdims=True))
        vt_s[j] = jnp.transpose(v_ref[rows, :].astype(F32)).astype(BF16)
    kmean = jnp.concatenate(kmeans + [jnp.zeros((16 - nb, LANES), F32)], axis=0).astype(BF16)

    jb = lax.broadcasted_iota(jnp.int32, (nb, T), 0)
    tb = lax.broadcasted_iota(jnp.int32, (nb, T), 1) // BLK
    past = jb < tb
    for hh in range(2):
        gt = lax.dot_general(kmean, qn_s[hh], NT_DIMS, preferred_element_type=F32)[0:nb, :]
        gm = jnp.where(past, gt, MASK_VALUE)
        cnt = jnp.zeros((nb, T), jnp.int32)
        for jp in range(nb):
            r = gm[jp:jp + 1, :]
            ahead = (r > gm) | ((r == gm) & (jp < jb))
            cnt = cnt + ahead.astype(jnp.int32)
        bias = jnp.where((cnt < MOBA_TOPK) & past, 0.0, MASK_VALUE)
        for i in range(nb):
            bias_s[hh, i] = bias[:, i * BLK:(i + 1) * BLK]

    srow = lax.broadcasted_iota(jnp.int32, (BLK, BLK), 0)
    tcol = lax.broadcasted_iota(jnp.int32, (BLK, BLK), 1)
    d0 = (tcol - srow).astype(F32)
    causal = tcol >= srow

    def qblock(i, carry):
        rows_i = pl.ds(pl.multiple_of(i * BLK, BLK), BLK)
        outs = []
        for hh in range(2):
            slope = sl_ref[0, hh:hh + 1, :][:, 0:1]
            sd0 = slope * d0
            qh = qn_s[hh, rows_i, :]
            vrows = slice(hh * AT_DH, (hh + 1) * AT_DH)

            s = lax.dot_general(kn_s[rows_i, :], qh, NT_DIMS, preferred_element_type=F32)
            s = jnp.where(causal, s - sd0, MASK_VALUE)
            m0 = jnp.max(s, axis=0, keepdims=True)
            p = jnp.exp(s - m0)
            l0 = jnp.sum(p, axis=0, keepdims=True)
            acc0 = jnp.dot(vt_s[i][vrows, :], p.astype(BF16), preferred_element_type=F32)

            def kblock(j, mla):
                m_run, l_run, acc = mla
                rows_j = pl.ds(pl.multiple_of(j * BLK, BLK), BLK)
                s = lax.dot_general(kn_s[rows_j, :], qh, NT_DIMS, preferred_element_type=F32)
                off = ((i - j) * BLK).astype(F32)
                brow = bias_s[hh, i, pl.ds(j, 1), :] - slope * off
                s = s - sd0 + brow
                m_new = jnp.maximum(m_run, jnp.max(s, axis=0, keepdims=True))
                alpha = jnp.exp(m_run - m_new)
                p = jnp.exp(s - m_new)
                l_new = alpha * l_run + jnp.sum(p, axis=0, keepdims=True)
                acc = acc * alpha + jnp.dot(vt_s[j][vrows, :], p.astype(BF16),
                                            preferred_element_type=F32)
                return m_new, l_new, acc

            _, l_fin, acc = lax.fori_loop(0, i, kblock, (m0, l0, acc0))
            outs.append(acc / l_fin)
        o = jnp.transpose(jnp.concatenate(outs, axis=0))
        g = g_ref[rows_i, :].astype(F32)
        o_ref[rows_i, :] = (o * _silu(g)).astype(BF16)
        return carry

    lax.fori_loop(0, nb, qblock, 0)


def _moba(pb, qg2, kg2, slopes, batch, seq):
    n = pb.shape[0]
    nb = seq // MOBA_BLOCK
    blk = lambda off: pl.BlockSpec((seq, LANES), lambda b, p: (b, off + p))
    vec = pl.BlockSpec((1, LANES), lambda b, p: (0, 0))
    return pl.pallas_call(
        _moba_kernel,
        grid=(batch, AT_HEADS // 2),
        in_specs=[blk(COL_AQ), blk(COL_AK), blk(COL_AV), blk(COL_AG), vec, vec,
                  pl.BlockSpec((1, 2, LANES), lambda b, p: (p, 0, 0))],
        out_specs=pl.BlockSpec((seq, LANES), lambda b, p: (b, p)),
        out_shape=jax.ShapeDtypeStruct((n, AT_WIDTH), BF16),
        scratch_shapes=[
            pltpu.VMEM((2, seq, LANES), BF16),
            pltpu.VMEM((seq, LANES), BF16),
            pltpu.VMEM((nb, LANES, MOBA_BLOCK), BF16),
            pltpu.VMEM((2, nb, nb, MOBA_BLOCK), F32),
        ],
        compiler_params=pltpu.CompilerParams(
            dimension_semantics=("arbitrary", "arbitrary"), vmem_limit_bytes=VMEM_LIMIT),
        name="moba",
    )(pb, pb, pb, pb, qg2, kg2, slopes)


def _out_kernel(x_ref, mh_ref, ma_ref, wh_ref, wa_ref, o_ref):
    acc = jnp.dot(mh_ref[...], wh_ref[...], preferred_element_type=F32)
    acc = acc + jnp.dot(ma_ref[...], wa_ref[...], preferred_element_type=F32)
    o_ref[...] = x_ref[...] + acc


def _out_proj(x2d, mh, ma, wh, wa):
    n = x2d.shape[0]
    return pl.pallas_call(
        _out_kernel,
        grid=(n // OUT_ROWS,),
        in_specs=[
            pl.BlockSpec((OUT_ROWS, D_MODEL), lambda i: (i, 0)),
            pl.BlockSpec((OUT_ROWS, HG_WIDTH), lambda i: (i, 0)),
            pl.BlockSpec((OUT_ROWS, AT_WIDTH), lambda i: (i, 0)),
            pl.BlockSpec((HG_WIDTH, D_MODEL), lambda i: (0, 0)),
            pl.BlockSpec((AT_WIDTH, D_MODEL), lambda i: (0, 0)),
        ],
        out_specs=pl.BlockSpec((OUT_ROWS, D_MODEL), lambda i: (i, 0)),
        out_shape=jax.ShapeDtypeStruct((n, D_MODEL), F32),
        compiler_params=pltpu.CompilerParams(
            dimension_semantics=("arbitrary",), vmem_limit_bytes=VMEM_LIMIT),
        name="out_proj_residual",
    )(x2d, mh, ma, wh, wa)


def _lower_bound_rows(lb_logits):
    p = jax.nn.softmax(lb_logits.astype(F32), axis=0)
    lb = jnp.clip(jnp.cumsum(p, axis=0) - p[0:1], 0.0, LB_MAX)
    return jnp.stack([lb, jnp.log(lb), jnp.log1p(-lb)], axis=1)


def kernel(x, norm_w, w_in, hg_lb_logits, hg_norm_w, q_norm_w, k_norm_w, w_out):
    batch, seq, _ = x.shape
    depth = norm_w.shape[0]
    lbs = _lower_bound_rows(hg_lb_logits)
    slopes = jnp.asarray([2.0 ** (-8.0 * (h + 1) / AT_HEADS) for h in range(AT_HEADS)], F32)
    slopes = jnp.broadcast_to(slopes.reshape(AT_HEADS // 2, 2, 1), (AT_HEADS // 2, 2, LANES))
    x2d = x.reshape(batch * seq, D_MODEL)
    for l in range(depth):
        pb, hf = _proj(x2d, norm_w[l][None, :], w_in[l].astype(BF16))
        mh = _hgrn(pb, hf, lbs[l], hg_norm_w[l][None, :], batch, seq)
        qg2 = jnp.tile(q_norm_w[l], 2)[None, :]
        kg2 = jnp.tile(k_norm_w[l], 2)[None, :]
        ma = _moba(pb, qg2, kg2, slopes, batch, seq)
        w_o = w_out[l].astype(BF16)
        x2d = _out_proj(x2d, mh, ma, w_o[:HG_WIDTH], w_o[HG_WIDTH:])
    return x2d.reshape(batch, seq, D_MODEL)
```

```python
import jax
import jax.numpy as jnp
from jax import lax
from jax.experimental import pallas as pl
from jax.experimental.pallas import tpu as pltpu

F32 = jnp.float32
BF16 = jnp.bfloat16

D_MODEL = 1024
HG_WIDTH = 512
HG_HEADS = 4
HG_D = 128
AT_WIDTH = 512
AT_HEADS = 8
AT_DH = 64
MOBA_BLOCK = 256
MOBA_TOPK = 3
NORM_EPS = 1e-6
MASK_VALUE = -1e30
LB_MAX = 1.0 - 1e-6

LANES = 128
SUBLANES = 8
BF16_ROWS = 16
HG_CHUNK = 128
PROJ_ROWS = 512
SEG = 512
OUT_ROWS = 1024
VMEM_LIMIT = 48 * 1024 * 1024

SEG_HQ, SEG_HF, SEG_HI, SEG_HG, SEG_AQ, SEG_AK, SEG_AV, SEG_AG = range(8)

NT_DIMS = (((1,), (1,)), ((), ()))
TN_DIMS = (((0,), (0,)), ((), ()))


def _sigmoid(x):
    e = jnp.exp(-jnp.abs(x))
    return jnp.where(x >= 0, 1.0, e) / (1.0 + e)


def _silu(x):
    return x * _sigmoid(x)


def _proj_kernel(x_ref, nw_ref, w_ref, wqt_ref, wvt_ref, qg_ref, kg_ref,
                 ph_ref, hf_ref, pa_ref, qt_ref, vt_ref, km_ref):
    x = x_ref[...]
    ms = jnp.mean(x * x, axis=-1, keepdims=True)
    h = (x * lax.rsqrt(ms + NORM_EPS) * nw_ref[...]).astype(BF16)

    def seg(s):
        return jnp.dot(h, w_ref[:, s * SEG:(s + 1) * SEG], preferred_element_type=F32)

    ph_ref[:, 0:SEG] = seg(0).astype(BF16)
    hf_ref[...] = seg(1)
    ph_ref[:, SEG:2 * SEG] = seg(2).astype(BF16)
    ph_ref[:, 2 * SEG:3 * SEG] = seg(3).astype(BF16)
    pa_ref[:, SEG:2 * SEG] = seg(5).astype(BF16)

    k = seg(4)
    lane = lax.broadcasted_iota(jnp.int32, (PROJ_ROWS, LANES), 1)
    head0 = lane < AT_DH
    for t in range(SEG // LANES):
        cols = slice(t * LANES, (t + 1) * LANES)
        kt = k[:, cols]
        k2 = kt * kt
        s0 = jnp.sum(jnp.where(head0, k2, 0.0), axis=-1, keepdims=True)
        s1 = jnp.sum(jnp.where(head0, 0.0, k2), axis=-1, keepdims=True)
        inv = jnp.where(head0, lax.rsqrt(s0 / AT_DH + NORM_EPS), lax.rsqrt(s1 / AT_DH + NORM_EPS))
        kn = kt * inv * kg_ref[:, cols]
        pa_ref[:, cols] = kn.astype(BF16)
        for blk in range(PROJ_ROWS // MOBA_BLOCK):
            rows = slice(blk * MOBA_BLOCK, (blk + 1) * MOBA_BLOCK)
            km_ref[0, blk:blk + 1, cols] = jnp.mean(kn[rows, :], axis=0, keepdims=True)

    qt = lax.dot_general(wqt_ref[...], h, NT_DIMS, preferred_element_type=F32)
    q3 = qt.reshape(AT_HEADS, AT_DH, PROJ_ROWS)
    inv = lax.rsqrt(jnp.mean(q3 * q3, axis=1, keepdims=True) + NORM_EPS)
    qt_ref[...] = ((q3 * inv).reshape(SEG, PROJ_ROWS) * qg_ref[...]).astype(BF16)

    vt_ref[...] = lax.dot_general(wvt_ref[...], h, NT_DIMS, preferred_element_type=F32).astype(BF16)


def _proj(x2d, nw, w_main, wqt, wvt, qg_col, kg_row):
    n = x2d.shape[0]
    nseg = w_main.shape[1] // SEG
    const = lambda shape: pl.BlockSpec(shape, lambda i: tuple(0 for _ in shape))
    return pl.pallas_call(
        _proj_kernel,
        grid=(n // PROJ_ROWS,),
        in_specs=[
            pl.BlockSpec((PROJ_ROWS, D_MODEL), lambda i: (i, 0)),
            const((1, D_MODEL)),
            const((D_MODEL, nseg * SEG)),
            const((SEG, D_MODEL)),
            const((SEG, D_MODEL)),
            const((SEG, 1)),
            const((1, SEG)),
        ],
        out_specs=[
            pl.BlockSpec((PROJ_ROWS, 3 * SEG), lambda i: (i, 0)),
            pl.BlockSpec((PROJ_ROWS, SEG), lambda i: (i, 0)),
            pl.BlockSpec((PROJ_ROWS, 2 * SEG), lambda i: (i, 0)),
            pl.BlockSpec((SEG, PROJ_ROWS), lambda i: (0, i)),
            pl.BlockSpec((SEG, PROJ_ROWS), lambda i: (0, i)),
            pl.BlockSpec((1, PROJ_ROWS // MOBA_BLOCK, SEG), lambda i: (i, 0, 0)),
        ],
        out_shape=[
            jax.ShapeDtypeStruct((n, 3 * SEG), BF16),
            jax.ShapeDtypeStruct((n, SEG), F32),
            jax.ShapeDtypeStruct((n, 2 * SEG), BF16),
            jax.ShapeDtypeStruct((SEG, n), BF16),
            jax.ShapeDtypeStruct((SEG, n), BF16),
            jax.ShapeDtypeStruct((n // PROJ_ROWS, PROJ_ROWS // MOBA_BLOCK, SEG), F32),
        ],
        compiler_params=pltpu.CompilerParams(
            dimension_semantics=("arbitrary",), vmem_limit_bytes=VMEM_LIMIT),
        name="norm_in_proj",
    )(x2d, nw, w_main, wqt, wvt, qg_col, kg_row)


def _split3(x):
    h1 = x.astype(BF16)
    r1 = x - h1.astype(F32)
    h2 = r1.astype(BF16)
    r2 = r1 - h2.astype(F32)
    return h1, h2, r2.astype(BF16)


def _boundary_rows(b, m):
    L = b.shape[0]
    nv = L // SUBLANES
    b3 = b.reshape(nv, SUBLANES, LANES)

    def sub(j):
        return jnp.broadcast_to(b3[:, j:j + 1, :], (nv, SUBLANES, LANES))

    if m >= SUBLANES:
        last = sub(SUBLANES - 1)
        half = m // SUBLANES
        parts = []
        for p in range(nv // (2 * half)):
            r = p * 2 * half + half - 1
            parts.append(jnp.broadcast_to(last[r:r + 1], (2 * half, SUBLANES, LANES)))
        e3 = parts[0] if len(parts) == 1 else jnp.concatenate(parts, axis=0)
    else:
        s = lax.broadcasted_iota(jnp.int32, (nv, SUBLANES, LANES), 1)
        e3 = sub(m - 1)
        for start in range(2 * m, SUBLANES, 2 * m):
            e3 = jnp.where(s >= start, sub(start + m - 1), e3)
    return e3.reshape(L, LANES)


def _hgrn_kernel(hq_ref, hf_ref, hi_ref, hg_ref, lb_ref, nw_ref, o_ref):
    T = hq_ref.shape[0]
    L = HG_CHUNK
    lb = lb_ref[0:1, :]
    log_lb = lb_ref[1:2, :]
    log1m_lb = lb_ref[2:3, :]
    nw = nw_ref[...]

    row = lax.broadcasted_iota(jnp.int32, (L, L), 0)
    col = lax.broadcasted_iota(jnp.int32, (L, L), 1)
    tri = (row >= col).astype(BF16)
    x = row ^ col
    level = jnp.full((L, L), -1, jnp.int32)
    m = 1
    while m < L:
        level = level + (x >= m).astype(jnp.int32)
        m *= 2
    level = jnp.where(col > row, -2, level)
    rowl = lax.broadcasted_iota(jnp.int32, (L, LANES), 0)

    def chunk(c, st):
        rows = pl.ds(pl.multiple_of(c * L, L), L)
        qr = hq_ref[rows, :].astype(F32)
        z = hf_ref[rows, :]
        v = hi_ref[rows, :]
        q = _silu(qr)
        e = jnp.exp(-jnp.abs(z))
        sig = jnp.where(z >= 0, 1.0, e) / (1.0 + e)
        log_sig = jnp.minimum(z, 0.0) - jnp.log(1.0 + e)
        f = lb + (1.0 - lb) * sig
        k = 1.0 - f
        a1 = log_lb
        a2 = log1m_lb + log_sig
        log_f = jnp.maximum(a1, a2) + jnp.log(1.0 + jnp.exp(-jnp.abs(a1 - a2)))

        h1, h2, h3 = _split3(log_f)
        b = (jnp.dot(tri, h1, preferred_element_type=F32)
             + jnp.dot(tri, h2, preferred_element_type=F32)
             + jnp.dot(tri, h3, preferred_element_type=F32))
        b_last = b[L - 1:L, :]

        q_in = (q * jnp.exp(b)).astype(BF16)
        o = lax.dot_general(q_in, st.astype(BF16), NT_DIMS, preferred_element_type=F32)

        a = jnp.where(level == -1, jnp.sum(q * k, axis=-1, keepdims=True), 0.0)
        m = 1
        lg = 0
        while m < L:
            w = jnp.exp(-jnp.abs(b - _boundary_rows(b, m)))
            second = (rowl & m) != 0
            qt = (q * jnp.where(second, w, 0.0)).astype(BF16)
            kt = (k * jnp.where(second, 0.0, w)).astype(BF16)
            al = lax.dot_general(qt, kt, NT_DIMS, preferred_element_type=F32)
            a = jnp.where(level == lg, al, a)
            m *= 2
            lg += 1
        o = o + jnp.dot(a.astype(BF16), v, preferred_element_type=F32)

        k_st = (k * jnp.exp(b_last - b)).astype(BF16)
        st = st * jnp.exp(b_last) + lax.dot_general(v, k_st, TN_DIMS, preferred_element_type=F32)

        ms = jnp.mean(o * o, axis=-1, keepdims=True)
        on = o * lax.rsqrt(ms + NORM_EPS) * nw
        g = hg_ref[rows, :].astype(F32)
        o_ref[rows, :] = (on * _silu(g)).astype(BF16)
        return st

    lax.fori_loop(0, T // L, chunk, jnp.zeros((HG_D, HG_D), F32))


def _hgrn(ph, hf, lbs, nw, batch, seq):
    n = ph.shape[0]
    heads_per_seg = SEG // LANES
    blk = lambda s: pl.BlockSpec((seq, LANES), lambda b, h: (b, s * heads_per_seg + h))
    return pl.pallas_call(
        _hgrn_kernel,
        grid=(batch, HG_HEADS),
        in_specs=[
            blk(0),
            pl.BlockSpec((seq, LANES), lambda b, h: (b, h)),
            blk(1),
            blk(2),
            pl.BlockSpec((3, LANES), lambda b, h: (0, h)),
            pl.BlockSpec((1, LANES), lambda b, h: (0, 0)),
        ],
        out_specs=pl.BlockSpec((seq, LANES), lambda b, h: (b, h)),
        out_shape=jax.ShapeDtypeStruct((n, HG_WIDTH), BF16),
        compiler_params=pltpu.CompilerParams(
            dimension_semantics=("arbitrary", "arbitrary"), vmem_limit_bytes=VMEM_LIMIT),
        name="hgrn2",
    )(ph, hf, ph, ph, lbs, nw)


AUG_ONEHOT = 0
AUG_KBLK, AUG_KOFF, AUG_ONE_A, AUG_ONE_B = 8, 9, 10, 11
V_ROWS = AT_DH + BF16_ROWS


def _moba_kernel(qt_ref, k_ref, vt_ref, g_ref, km_ref, sl_ref, o_ref, kaug_s, qaug_s, vaug_s):
    T = k_ref.shape[0]
    BLK = MOBA_BLOCK
    nb = T // BLK
    assert nb <= AUG_KBLK

    km = jnp.concatenate([km_ref[a] for a in range(km_ref.shape[0])]
                         + [jnp.zeros((BF16_ROWS - nb, LANES), F32)], axis=0)
    lane16 = lax.broadcasted_iota(jnp.int32, (BF16_ROWS, LANES), 1)
    jb = lax.broadcasted_iota(jnp.int32, (nb, T), 0)
    tpos = lax.broadcasted_iota(jnp.int32, (1, T), 1)
    tblk = tpos // BLK
    toff = (tpos % BLK).astype(F32)
    past = jb < tblk
    qt = qt_ref[...]
    for hh in range(2):
        in_head = (lane16 < AT_DH) if hh == 0 else (lane16 >= AT_DH)
        gate = jnp.dot(jnp.where(in_head, km, 0.0).astype(BF16), qt,
                       preferred_element_type=F32)[0:nb, :]
        gm = jnp.where(past, gate, MASK_VALUE)
        cnt = jnp.zeros((nb, T), jnp.int32)
        for jp in range(nb):
            r = gm[jp:jp + 1, :]
            ahead = (r > gm) | ((r == gm) & (jp < jb))
            cnt = cnt + ahead.astype(jnp.int32)
        chosen = ((cnt < MOBA_TOPK) & past) | (jb == tblk)
        bias = jnp.where(chosen, 0.0, MASK_VALUE)
        slope = sl_ref[0, hh:hh + 1, :][:, 0:1]
        srow = slope * jnp.ones((1, T), F32)
        aug = jnp.concatenate(
            ([bias] if nb == AUG_KBLK else [bias, jnp.zeros((AUG_KBLK - nb, T), F32)])
            + [srow, srow, -slope * (tblk * BLK).astype(F32), -slope * toff,
               jnp.zeros((AT_DH - AUG_KBLK - 4, T), F32)], axis=0).astype(BF16)
        own = qt[hh * AT_DH:(hh + 1) * AT_DH, :]
        qaug_s[hh] = jnp.concatenate([own, aug] if hh == 0 else [aug, own], axis=0)
        vaug_s[hh] = jnp.concatenate(
            [vt_ref[hh * AT_DH:(hh + 1) * AT_DH, :], jnp.ones((BF16_ROWS, T), BF16)], axis=0)

    lane = lax.broadcasted_iota(jnp.int32, (BLK, LANES), 1)
    koff = lax.broadcasted_iota(jnp.int32, (BLK, LANES), 0).astype(F32)
    for j in range(nb):
        rows = slice(j * BLK, (j + 1) * BLK)
        kn = k_ref[rows, :].astype(F32)
        for hh in range(2):
            a = lane - AT_DH if hh == 0 else lane
            in_aug = (lane >= AT_DH) if hh == 0 else (lane < AT_DH)
            augv = jnp.where(a == AUG_ONEHOT + j, 1.0,
                             jnp.where(a == AUG_KBLK, float(j * BLK),
                                       jnp.where(a == AUG_KOFF, koff,
                                                 jnp.where((a == AUG_ONE_A) | (a == AUG_ONE_B), 1.0, 0.0))))
            kaug_s[hh, rows, :] = jnp.where(in_aug, augv, kn).astype(BF16)

    srow_i = lax.broadcasted_iota(jnp.int32, (BLK, BLK), 0)
    tcol_i = lax.broadcasted_iota(jnp.int32, (BLK, BLK), 1)
    causal = tcol_i >= srow_i

    for i in range(nb):
        cols = slice(i * BLK, (i + 1) * BLK)
        kk = (i + 1) * BLK
        outs = []
        for hh in range(2):
            s_all = jnp.dot(kaug_s[hh, 0:kk, :], qaug_s[hh, :, cols], preferred_element_type=F32)
            s_diag = jnp.where(causal, s_all[i * BLK:kk, :], MASK_VALUE)
            m = jnp.max(s_diag, axis=0, keepdims=True)
            if i > 0:
                s_past = s_all[0:i * BLK, :]
                m = jnp.maximum(m, jnp.max(s_past, axis=0, keepdims=True))
                p = jnp.concatenate([jnp.exp(s_past - m).astype(BF16),
                                     jnp.exp(s_diag - m).astype(BF16)], axis=0)
            else:
                p = jnp.exp(s_diag - m).astype(BF16)
            acc = jnp.dot(vaug_s[hh, :, 0:kk], p, preferred_element_type=F32)
            outs.append(acc[0:AT_DH, :] / acc[AT_DH:AT_DH + 1, :])
        o = jnp.transpose(jnp.concatenate(outs, axis=0))
        g = g_ref[cols, :].astype(F32)
        o_ref[cols, :] = (o * _silu(g)).astype(BF16)


def _moba(qt, pa, vt, kmean, slopes, batch, seq):
    n = pa.shape[0]
    pairs = AT_HEADS // 2
    tiles = seq // PROJ_ROWS
    return pl.pallas_call(
        _moba_kernel,
        grid=(batch, pairs),
        in_specs=[
            pl.BlockSpec((LANES, seq), lambda b, p: (p, b)),
            pl.BlockSpec((seq, LANES), lambda b, p: (b, p)),
            pl.BlockSpec((LANES, seq), lambda b, p: (p, b)),
            pl.BlockSpec((seq, LANES), lambda b, p: (b, pairs + p)),
            pl.BlockSpec((tiles, PROJ_ROWS // MOBA_BLOCK, LANES), lambda b, p: (b, 0, p)),
            pl.BlockSpec((1, 2, LANES), lambda b, p: (p, 0, 0)),
        ],
        out_specs=pl.BlockSpec((seq, LANES), lambda b, p: (b, p)),
        out_shape=jax.ShapeDtypeStruct((n, AT_WIDTH), BF16),
        scratch_shapes=[
            pltpu.VMEM((2, seq, LANES), BF16),
            pltpu.VMEM((2, LANES, seq), BF16),
            pltpu.VMEM((2, V_ROWS, seq), BF16),
        ],
        compiler_params=pltpu.CompilerParams(
            dimension_semantics=("arbitrary", "arbitrary"), vmem_limit_bytes=VMEM_LIMIT),
        name="moba",
    )(qt, pa, vt, pa, kmean, slopes)


def _out_kernel(x_ref, mh_ref, ma_ref, wh_ref, wa_ref, o_ref):
    acc = jnp.dot(mh_ref[...], wh_ref[...], preferred_element_type=F32)
    acc = acc + jnp.dot(ma_ref[...], wa_ref[...], preferred_element_type=F32)
    o_ref[...] = x_ref[...] + acc


def _out_proj(x2d, mh, ma, wh, wa):
    n = x2d.shape[0]
    return pl.pallas_call(
        _out_kernel,
        grid=(n // OUT_ROWS,),
        in_specs=[
            pl.BlockSpec((OUT_ROWS, D_MODEL), lambda i: (i, 0)),
            pl.BlockSpec((OUT_ROWS, HG_WIDTH), lambda i: (i, 0)),
            pl.BlockSpec((OUT_ROWS, AT_WIDTH), lambda i: (i, 0)),
            pl.BlockSpec((HG_WIDTH, D_MODEL), lambda i: (0, 0)),
            pl.BlockSpec((AT_WIDTH, D_MODEL), lambda i: (0, 0)),
        ],
        out_specs=pl.BlockSpec((OUT_ROWS, D_MODEL), lambda i: (i, 0)),
        out_shape=jax.ShapeDtypeStruct((n, D_MODEL), F32),
        compiler_params=pltpu.CompilerParams(
            dimension_semantics=("arbitrary",), vmem_limit_bytes=VMEM_LIMIT),
        name="out_proj_residual",
    )(x2d, mh, ma, wh, wa)


def _lower_bound_rows(lb_logits):
    p = jax.nn.softmax(lb_logits.astype(F32), axis=0)
    lb = jnp.clip(jnp.cumsum(p, axis=0) - p[0:1], 0.0, LB_MAX)
    return jnp.stack([lb, jnp.log(lb), jnp.log1p(-lb)], axis=1)


def kernel(x, norm_w, w_in, hg_lb_logits, hg_norm_w, q_norm_w, k_norm_w, w_out):
    batch, seq, _ = x.shape
    assert seq % PROJ_ROWS == 0 and (batch * seq) % OUT_ROWS == 0
    depth = norm_w.shape[0]
    lbs = _lower_bound_rows(hg_lb_logits)
    slopes = jnp.asarray([2.0 ** (-8.0 * (h + 1) / AT_HEADS) for h in range(AT_HEADS)], F32)
    slopes = jnp.broadcast_to(slopes.reshape(AT_HEADS // 2, 2, 1), (AT_HEADS // 2, 2, LANES))
    x2d = x.reshape(batch * seq, D_MODEL)
    seg = lambda w, s: w[:, s * SEG:(s + 1) * SEG]
    for l in range(depth):
        w = w_in[l].astype(BF16)
        w_main = jnp.concatenate([seg(w, s) for s in (SEG_HQ, SEG_HF, SEG_HI, SEG_HG, SEG_AK, SEG_AG)], axis=1)
        qg_col = (jnp.tile(q_norm_w[l], AT_HEADS) * AT_DH ** -0.5)[:, None]
        kg_row = jnp.tile(k_norm_w[l], AT_HEADS)[None, :]
        ph, hf, pa, qt, vt, kmean = _proj(x2d, norm_w[l][None, :], w_main,
                                          seg(w, SEG_AQ).T, seg(w, SEG_AV).T, qg_col, kg_row)
        mh = _hgrn(ph, hf, lbs[l], hg_norm_w[l][None, :], batch, seq)
        ma = _moba(qt, pa, vt, kmean, slopes, batch, seq)
        w_o = w_out[l].astype(BF16)
        x2d = _out_proj(x2d, mh, ma, w_o[:HG_WIDTH], w_o[HG_WIDTH:])
    return x2d.reshape(batch, seq, D_MODEL)
```

```python
import jax
import jax.numpy as jnp
from jax import lax
from jax.experimental import pallas as pl
from jax.experimental.pallas import tpu as pltpu

F32 = jnp.float32
BF16 = jnp.bfloat16

D_MODEL = 1024
HG_WIDTH = 512
HG_HEADS = 4
HG_D = 128
AT_WIDTH = 512
AT_HEADS = 8
AT_DH = 64
MOBA_BLOCK = 256
MOBA_TOPK = 3
NORM_EPS = 1e-6
MASK_VALUE = -1e30
LB_MAX = 1.0 - 1e-6
LOG2_E = 1.4426950408889634

LANES = 128
SUBLANES = 8
BF16_ROWS = 16
HG_CHUNK = 128
PROJ_ROWS = 512
SEG = 512
OUT_ROWS = 1024
VMEM_LIMIT = 48 * 1024 * 1024

SEG_HQ, SEG_HF, SEG_HI, SEG_HG, SEG_AQ, SEG_AK, SEG_AV, SEG_AG = range(8)

NT_DIMS = (((1,), (1,)), ((), ()))
TN_DIMS = (((0,), (0,)), ((), ()))


def _sigmoid(x):
    e = jnp.exp(-jnp.abs(x))
    return jnp.where(x >= 0, 1.0, e) / (1.0 + e)


def _silu(x):
    return x * _sigmoid(x)


def _proj_kernel(x_ref, nw_ref, w_ref, wqt_ref, wvt_ref, qg_ref, kg_ref, lb_ref,
                 ph_ref, hf_ref, pa_ref, qt_ref, vt_ref, km_ref):
    x = x_ref[...]
    ms = jnp.mean(x * x, axis=-1, keepdims=True)
    h = (x * lax.rsqrt(ms + NORM_EPS) * nw_ref[...]).astype(BF16)

    def seg(s):
        return jnp.dot(h, w_ref[:, s * SEG:(s + 1) * SEG], preferred_element_type=F32)

    ph_ref[:, 0:SEG] = _silu(seg(0)).astype(BF16)
    ph_ref[:, SEG:2 * SEG] = seg(2).astype(BF16)
    ph_ref[:, 2 * SEG:3 * SEG] = _silu(seg(3)).astype(BF16)
    pa_ref[:, SEG:2 * SEG] = _silu(seg(5)).astype(BF16)

    z = seg(1)
    e = jnp.exp(-jnp.abs(z))
    a1 = lb_ref[1:2, :]
    a2 = lb_ref[2:3, :] + jnp.minimum(z, 0.0) - jnp.log(1.0 + e)
    log_f = jnp.maximum(a1, a2) + jnp.log(1.0 + jnp.exp(-jnp.abs(a1 - a2)))
    hf_ref[...] = log_f * LOG2_E

    k = seg(4)
    lane = lax.broadcasted_iota(jnp.int32, (PROJ_ROWS, LANES), 1)
    head0 = lane < AT_DH
    for t in range(SEG // LANES):
        cols = slice(t * LANES, (t + 1) * LANES)
        kt = k[:, cols]
        k2 = kt * kt
        s0 = jnp.sum(jnp.where(head0, k2, 0.0), axis=-1, keepdims=True)
        s1 = jnp.sum(jnp.where(head0, 0.0, k2), axis=-1, keepdims=True)
        inv = jnp.where(head0, lax.rsqrt(s0 / AT_DH + NORM_EPS), lax.rsqrt(s1 / AT_DH + NORM_EPS))
        kn = kt * inv * kg_ref[:, cols]
        pa_ref[:, cols] = kn.astype(BF16)
        for blk in range(PROJ_ROWS // MOBA_BLOCK):
            rows = slice(blk * MOBA_BLOCK, (blk + 1) * MOBA_BLOCK)
            km_ref[0, blk:blk + 1, cols] = jnp.mean(kn[rows, :], axis=0, keepdims=True)

    qt = lax.dot_general(wqt_ref[...], h, NT_DIMS, preferred_element_type=F32)
    q3 = qt.reshape(AT_HEADS, AT_DH, PROJ_ROWS)
    inv = lax.rsqrt(jnp.mean(q3 * q3, axis=1, keepdims=True) + NORM_EPS)
    qt_ref[...] = ((q3 * inv).reshape(SEG, PROJ_ROWS) * qg_ref[...]).astype(BF16)

    vt_ref[...] = lax.dot_general(wvt_ref[...], h, NT_DIMS, preferred_element_type=F32).astype(BF16)


def _proj(x2d, nw, w_main, wqt, wvt, qg_col, kg_row, lb_rows):
    n = x2d.shape[0]
    nseg = w_main.shape[1] // SEG
    const = lambda shape: pl.BlockSpec(shape, lambda i: tuple(0 for _ in shape))
    return pl.pallas_call(
        _proj_kernel,
        grid=(n // PROJ_ROWS,),
        in_specs=[
            pl.BlockSpec((PROJ_ROWS, D_MODEL), lambda i: (i, 0)),
            const((1, D_MODEL)),
            const((D_MODEL, nseg * SEG)),
            const((SEG, D_MODEL)),
            const((SEG, D_MODEL)),
            const((SEG, 1)),
            const((1, SEG)),
            const((3, SEG)),
        ],
        out_specs=[
            pl.BlockSpec((PROJ_ROWS, 3 * SEG), lambda i: (i, 0)),
            pl.BlockSpec((PROJ_ROWS, SEG), lambda i: (i, 0)),
            pl.BlockSpec((PROJ_ROWS, 2 * SEG), lambda i: (i, 0)),
            pl.BlockSpec((SEG, PROJ_ROWS), lambda i: (0, i)),
            pl.BlockSpec((SEG, PROJ_ROWS), lambda i: (0, i)),
            pl.BlockSpec((1, PROJ_ROWS // MOBA_BLOCK, SEG), lambda i: (i, 0, 0)),
        ],
        out_shape=[
            jax.ShapeDtypeStruct((n, 3 * SEG), BF16),
            jax.ShapeDtypeStruct((n, SEG), F32),
            jax.ShapeDtypeStruct((n, 2 * SEG), BF16),
            jax.ShapeDtypeStruct((SEG, n), BF16),
            jax.ShapeDtypeStruct((SEG, n), BF16),
            jax.ShapeDtypeStruct((n // PROJ_ROWS, PROJ_ROWS // MOBA_BLOCK, SEG), F32),
        ],
        compiler_params=pltpu.CompilerParams(
            dimension_semantics=("arbitrary",), vmem_limit_bytes=VMEM_LIMIT),
        name="norm_in_proj",
    )(x2d, nw, w_main, wqt, wvt, qg_col, kg_row, lb_rows)


def _split2(x):
    hi = x.astype(BF16)
    return hi, (x - hi.astype(F32)).astype(BF16)


def _sublane_rows(b3, j):
    return jnp.broadcast_to(b3[:, j:j + 1, :], b3.shape)


def _boundary_rows(b3, last, m):
    nv = b3.shape[0]
    if m >= SUBLANES:
        half = m // SUBLANES
        parts = []
        for p in range(nv // (2 * half)):
            r = p * 2 * half + half - 1
            parts.append(jnp.broadcast_to(last[r:r + 1], (2 * half,) + b3.shape[1:]))
        return parts[0] if len(parts) == 1 else jnp.concatenate(parts, axis=0)
    s = lax.broadcasted_iota(jnp.int32, b3.shape, 1)
    e3 = _sublane_rows(b3, m - 1)
    for start in range(2 * m, SUBLANES, 2 * m):
        e3 = jnp.where(s >= start, _sublane_rows(b3, start + m - 1), e3)
    return e3


def _hgrn_kernel(q_ref, lf_ref, v_ref, g_ref, nw_ref, o_ref, st_s):
    T = q_ref.shape[0]
    L = HG_CHUNK
    nv = L // SUBLANES
    nw = nw_ref[...]

    row = lax.broadcasted_iota(jnp.int32, (L, L), 0)
    col = lax.broadcasted_iota(jnp.int32, (L, L), 1)
    tri = (row >= col).astype(BF16)
    x = row ^ col
    level = jnp.full((L, L), -1, jnp.int32)
    m = 1
    while m < L:
        level = level + (x >= m).astype(jnp.int32)
        m *= 2
    level = jnp.where(col > row, -2, level)
    rowl = lax.broadcasted_iota(jnp.int32, (L, LANES), 0)
    levels = []
    m, lg = 2, 1
    while m < L:
        levels.append((m, lg, jnp.where((rowl & m) != 0, 1.0, -1.0)))
        m *= 2
        lg += 1

    st_s[...] = jnp.zeros(st_s.shape, F32)

    def chunk(c, carry):
        rows = pl.ds(pl.multiple_of(c * L, L), L)
        for hd in range(HG_HEADS):
            cols = slice(hd * HG_D, (hd + 1) * HG_D)
            q = q_ref[rows, cols].astype(F32)
            lf = lf_ref[rows, cols]
            v = v_ref[rows, cols]
            f = jnp.exp2(lf)
            k = 1.0 - f

            hi, lo = _split2(lf)
            b = jnp.dot(tri, hi, preferred_element_type=F32) + jnp.dot(tri, lo, preferred_element_type=F32)
            b_last = b[L - 1:L, :]

            st = st_s[hd]
            o = lax.dot_general((q * jnp.exp2(b)).astype(BF16), st.astype(BF16), NT_DIMS,
                                preferred_element_type=F32)

            a = jnp.where(level == -1, jnp.sum(q * k, axis=-1, keepdims=True), 0.0)
            u = jnp.where((rowl & 1) != 0, q * f, k).astype(BF16)
            a = jnp.where(level == 0, lax.dot_general(u, u, NT_DIMS, preferred_element_type=F32), a)
            b3 = b.reshape(nv, SUBLANES, LANES)
            last = _sublane_rows(b3, SUBLANES - 1)
            for m, lg, sgn in levels:
                e = _boundary_rows(b3, last, m).reshape(L, LANES)
                w = jnp.exp2((b - e) * sgn)
                u = (jnp.where(sgn > 0, q, k) * w).astype(BF16)
                a = jnp.where(level == lg, lax.dot_general(u, u, NT_DIMS, preferred_element_type=F32), a)
            o = o + jnp.dot(a.astype(BF16), v, preferred_element_type=F32)

            k_st = (k * jnp.exp2(b_last - b)).astype(BF16)
            st_s[hd] = st * jnp.exp2(b_last) + lax.dot_general(v, k_st, TN_DIMS, preferred_element_type=F32)

            ms = jnp.mean(o * o, axis=-1, keepdims=True)
            on = o * lax.rsqrt(ms + NORM_EPS) * nw
            o_ref[rows, cols] = (on * g_ref[rows, cols].astype(F32)).astype(BF16)
        return carry

    lax.fori_loop(0, T // L, chunk, 0)


def _hgrn(ph, lf2, nw, batch, seq):
    n = ph.shape[0]
    return pl.pallas_call(
        _hgrn_kernel,
        grid=(batch,),
        in_specs=[
            pl.BlockSpec((seq, SEG), lambda b: (b, 0)),
            pl.BlockSpec((seq, SEG), lambda b: (b, 0)),
            pl.BlockSpec((seq, SEG), lambda b: (b, 1)),
            pl.BlockSpec((seq, SEG), lambda b: (b, 2)),
            pl.BlockSpec((1, LANES), lambda b: (0, 0)),
        ],
        out_specs=pl.BlockSpec((seq, HG_WIDTH), lambda b: (b, 0)),
        out_shape=jax.ShapeDtypeStruct((n, HG_WIDTH), BF16),
        scratch_shapes=[pltpu.VMEM((HG_HEADS, HG_D, HG_D), F32)],
        compiler_params=pltpu.CompilerParams(
            dimension_semantics=("arbitrary",), vmem_limit_bytes=VMEM_LIMIT),
        name="hgrn2",
    )(ph, lf2, ph, ph, nw)


AUG_ONEHOT = 0
AUG_KBLK, AUG_KOFF, AUG_ONE_A, AUG_ONE_B = 8, 9, 10, 11
V_ROWS = AT_DH + BF16_ROWS


def _moba_kernel(qt_ref, k_ref, vt_ref, g_ref, km_ref, sl_ref, o_ref, kaug_s, w_s, vaug_s, s_s, p_s):
    T = k_ref.shape[0]
    BLK = MOBA_BLOCK
    nb = T // BLK
    assert nb <= AUG_KBLK

    km = jnp.concatenate([km_ref[a] for a in range(km_ref.shape[0])]
                         + [jnp.zeros((BF16_ROWS - nb, LANES), F32)], axis=0)
    lane16 = lax.broadcasted_iota(jnp.int32, (BF16_ROWS, LANES), 1)
    jb = lax.broadcasted_iota(jnp.int32, (nb, T), 0)
    tpos = lax.broadcasted_iota(jnp.int32, (1, T), 1)
    tblk = tpos // BLK
    toff = (tpos % BLK).astype(F32)
    past = jb < tblk
    qt = qt_ref[...]
    for hh in range(2):
        in_head = (lane16 < AT_DH) if hh == 0 else (lane16 >= AT_DH)
        gate = jnp.dot(jnp.where(in_head, km, 0.0).astype(BF16), qt,
                       preferred_element_type=F32)[0:nb, :]
        gm = jnp.where(past, gate, MASK_VALUE)
        cnt = jnp.zeros((nb, T), jnp.int32)
        for jp in range(nb):
            r = gm[jp:jp + 1, :]
            ahead = (r > gm) | ((r == gm) & (jp < jb))
            cnt = cnt + ahead.astype(jnp.int32)
        chosen = ((cnt < MOBA_TOPK) & past) | (jb == tblk)
        bias = jnp.where(chosen, 0.0, MASK_VALUE)
        slope = sl_ref[0, hh:hh + 1, :][:, 0:1]
        srow = slope * jnp.ones((1, T), F32)
        aug = jnp.concatenate(
            ([bias] if nb == AUG_KBLK else [bias, jnp.zeros((AUG_KBLK - nb, T), F32)])
            + [srow, srow, -slope * (tblk * BLK).astype(F32), -slope * toff,
               jnp.zeros((LANES - AUG_KBLK - 4, T), F32)], axis=0).astype(BF16)
        rows128 = lax.broadcasted_iota(jnp.int32, (LANES, T), 0)
        in_rows = (rows128 < AT_DH) if hh == 0 else (rows128 >= AT_DH)
        w = jnp.concatenate([jnp.where(in_rows, qt, jnp.zeros_like(qt)), aug], axis=0)
        for i in range(nb):
            w_s[:, (2 * i + hh) * BLK:(2 * i + hh + 1) * BLK] = w[:, i * BLK:(i + 1) * BLK]
        vaug_s[hh] = jnp.concatenate(
            [vt_ref[hh * AT_DH:(hh + 1) * AT_DH, :], jnp.ones((BF16_ROWS, T), BF16)], axis=0)

    lane = lax.broadcasted_iota(jnp.int32, (BLK, LANES), 1)
    koff = lax.broadcasted_iota(jnp.int32, (BLK, LANES), 0).astype(F32)
    for j in range(nb):
        rows = slice(j * BLK, (j + 1) * BLK)
        kaug_s[rows, 0:LANES] = k_ref[rows, :]
        augv = jnp.where(lane == AUG_ONEHOT + j, 1.0,
                         jnp.where(lane == AUG_KBLK, float(j * BLK),
                                   jnp.where(lane == AUG_KOFF, koff,
                                             jnp.where((lane == AUG_ONE_A) | (lane == AUG_ONE_B), 1.0, 0.0))))
        kaug_s[rows, LANES:2 * LANES] = augv.astype(BF16)

    srow_i = lax.broadcasted_iota(jnp.int32, (BLK, 2 * BLK), 0)
    tcol_i = lax.broadcasted_iota(jnp.int32, (BLK, 2 * BLK), 1) % BLK
    causal = tcol_i >= srow_i

    def scores(i):
        kk = (i + 1) * BLK
        s_s[i % 2, 0:kk, :] = jnp.dot(kaug_s[0:kk, :], w_s[:, 2 * i * BLK:2 * (i + 1) * BLK],
                                      preferred_element_type=F32)

    def softmax(i):
        slot = i % 2
        drows = slice(i * BLK, (i + 1) * BLK)
        s_diag = jnp.where(causal, s_s[slot, drows, :], MASK_VALUE)
        m = jnp.max(s_diag, axis=0, keepdims=True)
        for j in range(i):
            m = jnp.maximum(m, jnp.max(s_s[slot, j * BLK:(j + 1) * BLK, :], axis=0, keepdims=True))
        p_s[slot, drows, :] = jnp.exp(s_diag - m).astype(BF16)
        for j in range(i):
            rows = slice(j * BLK, (j + 1) * BLK)
            p_s[slot, rows, :] = jnp.exp(s_s[slot, rows, :] - m).astype(BF16)

    def values(i):
        kk = (i + 1) * BLK
        cols = slice(i * BLK, (i + 1) * BLK)
        outs = []
        for hh in range(2):
            acc = jnp.dot(vaug_s[hh, :, 0:kk], p_s[i % 2, 0:kk, hh * BLK:(hh + 1) * BLK],
                          preferred_element_type=F32)
            outs.append(acc[0:AT_DH, :] / acc[AT_DH:AT_DH + 1, :])
        o = jnp.transpose(jnp.concatenate(outs, axis=0))
        o_ref[cols, :] = (o * g_ref[cols, :].astype(F32)).astype(BF16)

    scores(0)
    for i in range(nb):
        if i + 1 < nb:
            scores(i + 1)
        softmax(i)
        values(i)


def _moba(qt, pa, vt, kmean, slopes, batch, seq):
    n = pa.shape[0]
    pairs = AT_HEADS // 2
    tiles = seq // PROJ_ROWS
    return pl.pallas_call(
        _moba_kernel,
        grid=(batch, pairs),
        in_specs=[
            pl.BlockSpec((LANES, seq), lambda b, p: (p, b)),
            pl.BlockSpec((seq, LANES), lambda b, p: (b, p)),
            pl.BlockSpec((LANES, seq), lambda b, p: (p, b)),
            pl.BlockSpec((seq, LANES), lambda b, p: (b, pairs + p)),
            pl.BlockSpec((tiles, PROJ_ROWS // MOBA_BLOCK, LANES), lambda b, p: (b, 0, p)),
            pl.BlockSpec((1, 2, LANES), lambda b, p: (p, 0, 0)),
        ],
        out_specs=pl.BlockSpec((seq, LANES), lambda b, p: (b, p)),
        out_shape=jax.ShapeDtypeStruct((n, AT_WIDTH), BF16),
        scratch_shapes=[
            pltpu.VMEM((seq, 2 * LANES), BF16),
            pltpu.VMEM((2 * LANES, 2 * seq), BF16),
            pltpu.VMEM((2, V_ROWS, seq), BF16),
            pltpu.VMEM((2, seq, 2 * MOBA_BLOCK), F32),
            pltpu.VMEM((2, seq, 2 * MOBA_BLOCK), BF16),
        ],
        compiler_params=pltpu.CompilerParams(
            dimension_semantics=("arbitrary", "arbitrary"), vmem_limit_bytes=VMEM_LIMIT),
        name="moba",
    )(qt, pa, vt, pa, kmean, slopes)


def _out_kernel(x_ref, mh_ref, ma_ref, wh_ref, wa_ref, o_ref):
    acc = jnp.dot(mh_ref[...], wh_ref[...], preferred_element_type=F32)
    acc = acc + jnp.dot(ma_ref[...], wa_ref[...], preferred_element_type=F32)
    o_ref[...] = x_ref[...] + acc


def _out_proj(x2d, mh, ma, wh, wa):
    n = x2d.shape[0]
    return pl.pallas_call(
        _out_kernel,
        grid=(n // OUT_ROWS,),
        in_specs=[
            pl.BlockSpec((OUT_ROWS, D_MODEL), lambda i: (i, 0)),
            pl.BlockSpec((OUT_ROWS, HG_WIDTH), lambda i: (i, 0)),
            pl.BlockSpec((OUT_ROWS, AT_WIDTH), lambda i: (i, 0)),
            pl.BlockSpec((HG_WIDTH, D_MODEL), lambda i: (0, 0)),
            pl.BlockSpec((AT_WIDTH, D_MODEL), lambda i: (0, 0)),
        ],
        out_specs=pl.BlockSpec((OUT_ROWS, D_MODEL), lambda i: (i, 0)),
        out_shape=jax.ShapeDtypeStruct((n, D_MODEL), F32),
        compiler_params=pltpu.CompilerParams(
            dimension_semantics=("arbitrary",), vmem_limit_bytes=VMEM_LIMIT),
        name="out_proj_residual",
    )(x2d, mh, ma, wh, wa)


def _lower_bound_rows(lb_logits):
    p = jax.nn.softmax(lb_logits.astype(F32), axis=0)
    lb = jnp.clip(jnp.cumsum(p, axis=0) - p[0:1], 0.0, LB_MAX)
    return jnp.stack([lb, jnp.log(lb), jnp.log1p(-lb)], axis=1)


def kernel(x, norm_w, w_in, hg_lb_logits, hg_norm_w, q_norm_w, k_norm_w, w_out):
    batch, seq, _ = x.shape
    assert seq % PROJ_ROWS == 0 and (batch * seq) % OUT_ROWS == 0
    depth = norm_w.shape[0]
    lbs = _lower_bound_rows(hg_lb_logits)
    slopes = jnp.asarray([2.0 ** (-8.0 * (h + 1) / AT_HEADS) for h in range(AT_HEADS)], F32)
    slopes = jnp.broadcast_to(slopes.reshape(AT_HEADS // 2, 2, 1), (AT_HEADS // 2, 2, LANES))
    x2d = x.reshape(batch * seq, D_MODEL)
    seg = lambda w, s: w[:, s * SEG:(s + 1) * SEG]
    for l in range(depth):
        w = w_in[l].astype(BF16)
        w_main = jnp.concatenate([seg(w, s) for s in (SEG_HQ, SEG_HF, SEG_HI, SEG_HG, SEG_AK, SEG_AG)], axis=1)
        qg_col = (jnp.tile(q_norm_w[l], AT_HEADS) * AT_DH ** -0.5)[:, None]
        kg_row = jnp.tile(k_norm_w[l], AT_HEADS)[None, :]
        ph, lf2, pa, qt, vt, kmean = _proj(x2d, norm_w[l][None, :], w_main,
                                           seg(w, SEG_AQ).T, seg(w, SEG_AV).T, qg_col, kg_row, lbs[l])
        mh = _hgrn(ph, lf2, hg_norm_w[l][None, :], batch, seq)
        ma = _moba(qt, pa, vt, kmean, slopes, batch, seq)
        w_o = w_out[l].astype(BF16)
        x2d = _out_proj(x2d, mh, ma, w_o[:HG_WIDTH], w_o[HG_WIDTH:])
    return x2d.reshape(batch, seq, D_MODEL)
```
